```python
import math
import jax, jax.numpy as jnp
from jax import lax
import numpy as np

D_MODEL = 1024
BATCH = 8
SEQ = 4096
DEPTH = 1

CHUNK = 64
Q_BLOCK = 128
D_MIX = D_MODEL
MLA_HEADS = 8
NOPE_DIM = 64
ROPE_DIM = 32
V_DIM = 64
Q_LORA = 256
KV_LORA = 128
ROPE_THETA = 10000.0
MLA_WIDTH = MLA_HEADS * V_DIM
ATTN_SCALE = (NOPE_DIM + ROPE_DIM) ** -0.5
GM_HEADS = 8
GM_DIM = 64
GM_CHUNK = 128
GM_WIDTH = GM_HEADS * GM_DIM
IN_COLS = Q_LORA + KV_LORA + ROPE_DIM + 2 * GM_WIDTH
D_FF = 2816
CONV_W = 3
EPS = 1e-6

kernel_name = "hybrid_mla_gmlp_convffn_block"


def rms_norm(x, g):
    x32 = x.astype(jnp.float32)
    y = x32 * lax.rsqrt(jnp.mean(x32 * x32, axis=-1, keepdims=True) + EPS)
    return (y * g.astype(jnp.float32)).astype(x.dtype)


def layer_norm(x, g, b):
    x32 = x.astype(jnp.float32)
    mu = jnp.mean(x32, axis=-1, keepdims=True)
    var = jnp.mean(jnp.square(x32 - mu), axis=-1, keepdims=True)
    y = (x32 - mu) * lax.rsqrt(var + EPS)
    return (y * g.astype(jnp.float32) + b.astype(jnp.float32)).astype(x.dtype)


def modulate(h, shift, scale):
    return h * (1.0 + scale[:, None, :]) + shift[:, None, :]


def rope_tables(seq):
    pos = jnp.arange(seq, dtype=jnp.float32)
    inv = ROPE_THETA ** (-jnp.arange(0, ROPE_DIM, 2, dtype=jnp.float32) / ROPE_DIM)
    ang = pos[:, None] * inv[None, :]
    return jnp.cos(ang), jnp.sin(ang)


def apply_rope(t, cos, sin):
    t32 = t.astype(jnp.float32)
    t1, t2 = jnp.split(t32, 2, axis=-1)
    out = jnp.concatenate([t1 * cos - t2 * sin, t2 * cos + t1 * sin], axis=-1)
    return out.astype(t.dtype)


def mla_attention(q_nope, q_rope, k_nope, k_rope, v):
    B, S, H, _ = q_nope.shape
    nb = S // Q_BLOCK
    k_chunk = jnp.arange(S) // CHUNK

    def to_blocks(t):
        return t.reshape((B, nb, Q_BLOCK) + t.shape[2:]).swapaxes(0, 1)

    def one_block(args):
        qn, qr, bi = args
        s = (jnp.einsum('bqhd,bkhd->bhqk', qn, k_nope)
             + jnp.einsum('bqhr,bkr->bhqk', qr, k_rope))
        s = s.astype(jnp.float32) * ATTN_SCALE
        q_chunk = (bi * Q_BLOCK + jnp.arange(Q_BLOCK)) // CHUNK
        mask = k_chunk[None, :] <= q_chunk[:, None]
        s = jnp.where(mask[None, None], s, -jnp.inf)
        p = jax.nn.softmax(s, axis=-1).astype(v.dtype)
        return jnp.einsum('bhqk,bkhd->bqhd', p, v)

    o = lax.map(one_block, (to_blocks(q_nope), to_blocks(q_rope), jnp.arange(nb)))
    return o.swapaxes(0, 1).reshape(B, S, H * V_DIM)


def gmlp_spatial_gate(u, v, ln_g, ln_b, w_s, b_s):
    B, S, H, Dh = v.shape
    v = layer_norm(v, ln_g, ln_b)
    idx = jnp.arange(GM_CHUNK) // CHUNK
    mask = (idx[None, :] <= idx[:, None]).astype(w_s.dtype)
    w_m = w_s * mask[None]
    vb = v.reshape(B, S // GM_CHUNK, GM_CHUNK, H, Dh)
    mixed = jnp.einsum('hij,bnjhd->bnihd', w_m, vb) + b_s.T[None, None, :, :, None]
    return u * mixed.reshape(B, S, H, Dh)


def token_mixer(h, w_in, g_q, w_uq, g_kv, w_ukv, gm_ln_g, gm_ln_b, w_spatial, b_spatial, w_out):
    B, S, _ = h.shape
    z = h @ w_in
    o1 = Q_LORA
    o2 = o1 + KV_LORA
    o3 = o2 + ROPE_DIM
    c_q, c_kv, k_r, g_uv = z[..., :o1], z[..., o1:o2], z[..., o2:o3], z[..., o3:]
    q = (rms_norm(c_q, g_q) @ w_uq).reshape(B, S, MLA_HEADS, NOPE_DIM + ROPE_DIM)
    kv = (rms_norm(c_kv, g_kv) @ w_ukv).reshape(B, S, MLA_HEADS, NOPE_DIM + V_DIM)
    q_nope, q_rope = q[..., :NOPE_DIM], q[..., NOPE_DIM:]
    k_nope, v = kv[..., :NOPE_DIM], kv[..., NOPE_DIM:]
    cos, sin = rope_tables(S)
    q_rope = apply_rope(q_rope, cos[None, :, None, :], sin[None, :, None, :])
    k_rope = apply_rope(k_r, cos[None], sin[None])
    attn = mla_attention(q_nope, q_rope, k_nope, k_rope, v)
    g = jax.nn.gelu(g_uv)
    u = g[..., :GM_WIDTH].reshape(B, S, GM_HEADS, GM_DIM)
    vv = g[..., GM_WIDTH:].reshape(B, S, GM_HEADS, GM_DIM)
    sgu = gmlp_spatial_gate(u, vv, gm_ln_g, gm_ln_b, w_spatial, b_spatial).reshape(B, S, GM_WIDTH)
    return jnp.concatenate([attn, sgu], axis=-1) @ w_out


def conv_ffn(h, w_up, conv_w, conv_b, w_down):
    S = h.shape[1]
    up = h @ w_up
    upp = jnp.pad(up, ((0, 0), (CONV_W - 1, 0), (0, 0)))
    y = conv_b + sum(upp[:, k:k + S, :] * conv_w[k] for k in range(CONV_W))
    a, b = y[..., :D_FF], y[..., D_FF:]
    return (jax.nn.silu(a) * b) @ w_down


def setup_inputs(seed: int = 0) -> dict:
    key = jax.random.key(seed)
    ks = jax.random.split(key, 24)
    f32 = jnp.float32
    n = lambda k, shape, s: jax.random.normal(k, shape, f32) * s
    gain = lambda k, shape: 1.0 + 0.05 * jax.random.normal(k, shape, f32)
    L = DEPTH
    return {
        "x": jax.random.normal(ks[0], (BATCH, SEQ, D_MODEL), f32),
        "c": jax.random.normal(ks[1], (BATCH, D_MODEL), f32),
        "w_ada": n(ks[2], (L, D_MODEL, 6 * D_MODEL), 0.5 * D_MODEL ** -0.5),
        "b_ada": n(ks[3], (L, 6 * D_MODEL), 0.02),
        "g_pre_mix": gain(ks[4], (L, D_MODEL)),
        "g_post_mix": gain(ks[5], (L, D_MODEL)),
        "w_in": n(ks[6], (L, D_MODEL, IN_COLS), D_MODEL ** -0.5),
        "g_q": gain(ks[7], (L, Q_LORA)),
        "w_uq": n(ks[8], (L, Q_LORA, MLA_HEADS * (NOPE_DIM + ROPE_DIM)), Q_LORA ** -0.5),
        "g_kv": gain(ks[9], (L, KV_LORA)),
        "w_ukv": n(ks[10], (L, KV_LORA, MLA_HEADS * (NOPE_DIM + V_DIM)), KV_LORA ** -0.5),
        "gm_ln_g": gain(ks[11], (L, GM_HEADS, GM_DIM)),
        "gm_ln_b": n(ks[12], (L, GM_HEADS, GM_DIM), 0.02),
        "w_spatial": n(ks[13], (L, GM_HEADS, GM_CHUNK, GM_CHUNK), GM_CHUNK ** -0.5),
        "b_spatial": 1.0 + n(ks[14], (L, GM_HEADS, GM_CHUNK), 0.05),
        "w_out": n(ks[15], (L, D_MIX, D_MODEL), D_MIX ** -0.5),
        "g_pre_ffn": gain(ks[16], (L, D_MODEL)),
        "g_post_ffn": gain(ks[17], (L, D_MODEL)),
        "w_up": n(ks[18], (L, D_MODEL, 2 * D_FF), D_MODEL ** -0.5),
        "conv_w": n(ks[19], (L, CONV_W, 2 * D_FF), CONV_W ** -0.5),
        "conv_b": n(ks[20], (L, 2 * D_FF), 0.02),
        "w_down": n(ks[21], (L, D_FF, D_MODEL), D_FF ** -0.5),
    }


def reference(x, c, w_ada, b_ada, g_pre_mix, g_post_mix, w_in, g_q, w_uq, g_kv, w_ukv,
              gm_ln_g, gm_ln_b, w_spatial, b_spatial, w_out, g_pre_ffn, g_post_ffn,
              w_up, conv_w, conv_b, w_down):
    c_act = jax.nn.silu(c)
    for l in range(DEPTH):
        ada = c_act @ w_ada[l] + b_ada[l]
        sh1, sc1, gt1, sh2, sc2, gt2 = jnp.split(ada, 6, axis=-1)
        h = modulate(rms_norm(x, g_pre_mix[l]), sh1, sc1)
        m = token_mixer(h, w_in[l], g_q[l], w_uq[l], g_kv[l], w_ukv[l], gm_ln_g[l], gm_ln_b[l],
                        w_spatial[l], b_spatial[l], w_out[l])
        x = x + gt1[:, None, :] * rms_norm(m, g_post_mix[l])
        h = modulate(rms_norm(x, g_pre_ffn[l]), sh2, sc2)
        f = conv_ffn(h, w_up[l], conv_w[l], conv_b[l], w_down[l])
        x = x + gt2[:, None, :] * rms_norm(f, g_post_ffn[l])
    return x
```

```python
import functools
import math

import jax
import jax.numpy as jnp
import numpy as np
from jax import lax
from jax.experimental import pallas as pl
from jax.experimental.pallas import tpu as pltpu

D_MODEL = 1024
CHUNK = 64
MLA_HEADS = 8
NOPE_DIM = 64
ROPE_DIM = 32
V_DIM = 64
Q_LORA = 256
KV_LORA = 128
ROPE_THETA = 10000.0
ATTN_SCALE = (NOPE_DIM + ROPE_DIM) ** -0.5
GM_HEADS = 8
GM_DIM = 64
GM_CHUNK = 128
GM_WIDTH = GM_HEADS * GM_DIM
D_FF = 2816
CONV_W = 3
EPS = 1e-6

LANES = 128
SUBLANES = 8
HEAD_PAD = LANES
HALF_ROPE = ROPE_DIM // 2
LOG2E = math.log2(math.e)
VMEM_LIMIT = 56 * 1024 * 1024

BF16 = jnp.bfloat16
F32 = jnp.float32


def _rms(x, g):
    return x * lax.rsqrt(jnp.mean(x * x, axis=-1, keepdims=True) + EPS) * g


def _dot(a, b):
    return jnp.dot(a, b, preferred_element_type=F32)


def _ada_kernel(c_ref, w_ref, b_ref, o_ref):
    c = c_ref[...]
    c_act = c * jax.nn.sigmoid(c)
    o_ref[...] = jnp.dot(c_act, w_ref[...], preferred_element_type=F32,
                         precision=lax.Precision.HIGHEST) + b_ref[...]


def _ada(c, w_ada, b_ada):
    bsz, d = c.shape
    n = w_ada.shape[1]
    tn = 1536
    return pl.pallas_call(
        _ada_kernel,
        grid=(n // tn,),
        in_specs=[pl.BlockSpec((bsz, d), lambda j: (0, 0)),
                  pl.BlockSpec((d, tn), lambda j: (0, j)),
                  pl.BlockSpec((1, tn), lambda j: (0, j))],
        out_specs=pl.BlockSpec((bsz, tn), lambda j: (0, j)),
        out_shape=jax.ShapeDtypeStruct((bsz, n), F32),
        compiler_params=pltpu.CompilerParams(vmem_limit_bytes=VMEM_LIMIT),
        name="ada",
    )(c, w_ada, b_ada.reshape(1, n))


def _rope(t, c_tab, s1_tab, s2_tab):
    return (t * c_tab + pltpu.roll(t, HALF_ROPE, 1) * s1_tab
            + pltpu.roll(t, LANES - HALF_ROPE, 1) * s2_tab)


def _gelu_tanh(x):
    k = math.sqrt(2.0 / math.pi)
    return x * (0.5 * (1.0 + jnp.tanh(k * (x + 0.044715 * (x * x * x)))))


def _premix_kernel(x_ref, sh_ref, sc_ref, gpre_ref, win_ref, gq_ref, wuq_ref,
                   gkv_ref, wkv_ref, cq_ref, s1q_ref, s2q_ref, ck_ref, s1k_ref,
                   s2k_ref, aln_ref, lng_ref, lnb_ref, wsp_ref, bsp_ref,
                   q_ref, k_ref, v_ref, sgu_ref):
    tm = x_ref.shape[1]
    x = x_ref[0]
    h = _rms(x, gpre_ref[...]) * (1.0 + sc_ref[0]) + sh_ref[0]
    z = _dot(h.astype(BF16), win_ref[...])
    o1 = Q_LORA
    o2 = o1 + KV_LORA
    o3 = o2 + HEAD_PAD
    o4 = o3 + GM_WIDTH
    cqn = _rms(z[:, :o1], gq_ref[...]).astype(BF16)
    ckvn = _rms(z[:, o1:o2], gkv_ref[...]).astype(BF16)
    q = _dot(cqn, wuq_ref[...])
    kv = _dot(ckvn, wkv_ref[...])
    krr = _rope(z[:, o2:o3], ck_ref[...], s1k_ref[...], s2k_ref[...])
    cq, s1q, s2q = cq_ref[...], s1q_ref[...], s2q_ref[...]
    for hd in range(MLA_HEADS):
        sl = slice(hd * HEAD_PAD, (hd + 1) * HEAD_PAD)
        q_ref[0, :, sl] = _rope(q[:, sl], cq, s1q, s2q).astype(BF16)
        k_ref[0, :, sl] = (kv[:, sl] + krr).astype(BF16)
    v_ref[0] = kv[:, MLA_HEADS * HEAD_PAD:].astype(BF16)

    u = _gelu_tanh(z[:, o3:o4])
    vv = _gelu_tanh(z[:, o4:])
    aln = aln_ref[...]
    mu = _dot(vv.astype(BF16), aln)
    d = vv - mu
    var = _dot((d * d).astype(BF16), aln)
    y = (d * lax.rsqrt(var + EPS) * lng_ref[...] + lnb_ref[...]).astype(BF16)

    row = lax.broadcasted_iota(jnp.int32, (GM_CHUNK, GM_CHUNK), 0) // CHUNK
    col = lax.broadcasted_iota(jnp.int32, (GM_CHUNK, GM_CHUNK), 1) // CHUNK
    causal = col <= row
    lane = lax.broadcasted_iota(jnp.int32, (GM_CHUNK, LANES), 1)
    first_head = lane < GM_DIM
    bsp = bsp_ref[...]
    for pr in range(GM_HEADS // 2):
        wa = jnp.where(causal, wsp_ref[2 * pr], 0.0)
        wb = jnp.where(causal, wsp_ref[2 * pr + 1], 0.0)
        wpair = jnp.concatenate([wa, wb], axis=0).astype(BF16)
        ls = slice(pr * LANES, (pr + 1) * LANES)
        for blk in range(tm // GM_CHUNK):
            rs = slice(blk * GM_CHUNK, (blk + 1) * GM_CHUNK)
            mixed2 = _dot(wpair, y[rs, ls])
            mixed = jnp.where(first_head, mixed2[:GM_CHUNK], mixed2[GM_CHUNK:])
            sgu_ref[0, rs, ls] = (u[rs, ls] * (mixed + bsp[:, ls])).astype(BF16)


def _premix(x, sh1, sc1, g_pre, w_in_r, g_q, w_uq_p, g_kv, w_kv_p, tabs, a_ln,
            ln_g, ln_b, w_sp, b_sp, tm):
    bsz, seq, d = x.shape
    n_in = w_in_r.shape[1]
    qw = MLA_HEADS * HEAD_PAD
    const = lambda *shape: pl.BlockSpec(shape, lambda s, b: (0,) * len(shape))
    tab_spec = pl.BlockSpec((tm, LANES), lambda s, b: (s, 0))
    row_spec = lambda w: pl.BlockSpec((1, tm, w), lambda s, b: (b, s, 0))
    mod_spec = pl.BlockSpec((1, 1, d), lambda s, b: (b, 0, 0))
    return pl.pallas_call(
        _premix_kernel,
        grid=(seq // tm, bsz),
        in_specs=[row_spec(d), mod_spec, mod_spec, const(1, d), const(d, n_in),
                  const(1, Q_LORA), const(Q_LORA, qw), const(1, KV_LORA),
                  const(KV_LORA, qw + MLA_HEADS * V_DIM)]
                 + [tab_spec] * 6
                 + [const(GM_WIDTH, GM_WIDTH), const(1, GM_WIDTH), const(1, GM_WIDTH),
                    const(GM_HEADS, GM_CHUNK, GM_CHUNK), const(GM_CHUNK, GM_WIDTH)],
        out_specs=[row_spec(qw), row_spec(qw), row_spec(MLA_HEADS * V_DIM),
                   row_spec(GM_WIDTH)],
        out_shape=[jax.ShapeDtypeStruct((bsz, seq, qw), BF16),
                   jax.ShapeDtypeStruct((bsz, seq, qw), BF16),
                   jax.ShapeDtypeStruct((bsz, seq, MLA_HEADS * V_DIM), BF16),
                   jax.ShapeDtypeStruct((bsz, seq, GM_WIDTH), BF16)],
        compiler_params=pltpu.CompilerParams(
            dimension_semantics=("arbitrary", "arbitrary"),
            vmem_limit_bytes=VMEM_LIMIT),
        name="premix",
    )(x, sh1, sc1, g_pre, w_in_r, g_q, w_uq_p, g_kv, w_kv_p, *tabs, a_ln,
      ln_g, ln_b, w_sp, b_sp)


def _attn_kernel(q_ref, k_ref, v_ref, o_ref, m_sc, l_sc, acc_sc, *, tq, tk):
    qi = pl.program_id(2)
    lane = lax.broadcasted_iota(jnp.int32, (tq, LANES), 1)
    first_head = lane < V_DIM
    reps = tk // LANES

    m_sc[...] = jnp.full(m_sc.shape, -jnp.inf, F32)
    l_sc[...] = jnp.zeros(l_sc.shape, F32)
    acc_sc[...] = jnp.zeros(acc_sc.shape, F32)

    def step(k_start, mask):
        kblk = k_ref[0, pl.ds(k_start, tk), :]
        vblk = v_ref[0, pl.ds(k_start, tk), :]
        pvs, alphas = [], []
        for hh in range(2):
            hs = slice(hh * HEAD_PAD, (hh + 1) * HEAD_PAD)
            s = lax.dot_general(q_ref[0, :, hs], kblk[:, hs],
                                (((1,), (1,)), ((), ())),
                                preferred_element_type=F32)
            if mask is not None:
                s = jnp.where(mask, s, -jnp.inf)
            m_prev = m_sc[hh]
            m_new = jnp.maximum(m_prev, jnp.max(s, axis=1, keepdims=True))
            alpha = jnp.exp2(m_prev - m_new)
            p = jnp.exp2(s - pltpu.repeat(m_new, reps, axis=1))
            l_sc[hh] = alpha * l_sc[hh] + jnp.sum(p, axis=1, keepdims=True)
            m_sc[hh] = m_new
            pvs.append(_dot(p.astype(BF16), vblk))
            alphas.append(alpha)
        acc_sc[...] = (acc_sc[...] * jnp.where(first_head, alphas[0], alphas[1])
                       + jnp.where(first_head, pvs[0], pvs[1]))

    qc = lax.broadcasted_iota(jnp.int32, (tq, tk), 0) // CHUNK
    kc = lax.broadcasted_iota(jnp.int32, (tq, tk), 1) // CHUNK
    step(pl.multiple_of(qi * tq, tq), kc <= qc)

    def body(j, carry):
        step(pl.multiple_of(j * tk, tk), None)
        return carry

    lax.fori_loop(0, qi * (tq // tk), body, 0)

    inv = jnp.where(first_head, 1.0 / l_sc[0], 1.0 / l_sc[1])
    o_ref[0] = (acc_sc[...] * inv).astype(o_ref.dtype)


def _attention(q, k, v, tq, tk):
    bsz, seq, _ = q.shape
    pairs = MLA_HEADS // 2
    kern = functools.partial(_attn_kernel, tq=tq, tk=tk)
    return pl.pallas_call(
        kern,
        grid=(bsz, pairs, seq // tq),
        in_specs=[pl.BlockSpec((1, tq, 2 * HEAD_PAD), lambda b, p, i: (b, i, p)),
                  pl.BlockSpec((1, seq, 2 * HEAD_PAD), lambda b, p, i: (b, 0, p)),
                  pl.BlockSpec((1, seq, 2 * V_DIM), lambda b, p, i: (b, 0, p))],
        out_specs=pl.BlockSpec((1, tq, 2 * V_DIM), lambda b, p, i: (b, i, p)),
        out_shape=jax.ShapeDtypeStruct((bsz, seq, MLA_HEADS * V_DIM), BF16),
        scratch_shapes=[pltpu.VMEM((2, tq, LANES), F32),
                        pltpu.VMEM((2, tq, LANES), F32),
                        pltpu.VMEM((tq, LANES), F32)],
        compiler_params=pltpu.CompilerParams(
            dimension_semantics=("arbitrary", "arbitrary", "arbitrary"),
            vmem_limit_bytes=VMEM_LIMIT),
        name="attn",
    )(q, k, v)


def _postmix_kernel(x_ref, a_ref, g_ref, wo_ref, gpost_ref, gt_ref, gffn_ref,
                    sh_ref, sc_ref, x1_ref, h2_ref):
    half = a_ref.shape[2]
    m = _dot(a_ref[0], wo_ref[:half, :]) + _dot(g_ref[0], wo_ref[half:, :])
    x1 = x_ref[0] + gt_ref[0] * _rms(m, gpost_ref[...])
    x1_ref[0] = x1
    h2 = _rms(x1, gffn_ref[...]) * (1.0 + sc_ref[0]) + sh_ref[0]
    h2_ref[0] = h2.astype(BF16)


def _postmix(x, attn, sgu, w_out, g_post, gt1, g_ffn, sh2, sc2, tm):
    bsz, seq, d = x.shape
    const = lambda *shape: pl.BlockSpec(shape, lambda b, s: (0,) * len(shape))
    row_spec = lambda w: pl.BlockSpec((1, tm, w), lambda b, s: (b, s, 0))
    mod_spec = pl.BlockSpec((1, 1, d), lambda b, s: (b, 0, 0))
    return pl.pallas_call(
        _postmix_kernel,
        grid=(bsz, seq // tm),
        in_specs=[row_spec(d), row_spec(attn.shape[2]), row_spec(sgu.shape[2]),
                  const(d, d), const(1, d), mod_spec, const(1, d), mod_spec, mod_spec],
        out_specs=[row_spec(d), row_spec(d)],
        out_shape=[jax.ShapeDtypeStruct((bsz, seq, d), F32),
                   jax.ShapeDtypeStruct((bsz, seq, d), BF16)],
        compiler_params=pltpu.CompilerParams(
            dimension_semantics=("arbitrary", "arbitrary"),
            vmem_limit_bytes=VMEM_LIMIT),
        name="postmix",
    )(x, attn, sgu, w_out, g_post, gt1, g_ffn, sh2, sc2)


def _ffn_kernel(x1_ref, h2_ref, wup_ref, cw_ref, cb_ref, wdn_ref, gpost_ref,
                gt_ref, o_ref, carry_sc, acc_sc, *, n_chunks, tf):
    tm = h2_ref.shape[1]

    @pl.when(pl.program_id(1) == 0)
    def _():
        carry_sc[...] = jnp.zeros(carry_sc.shape, F32)

    h2 = h2_ref[0]
    acc_sc[...] = jnp.zeros(acc_sc.shape, F32)

    def chunk(c, carry):
        up = _dot(h2, wup_ref[c])
        ext = jnp.concatenate([carry_sc[c], up], axis=0)
        carry_sc[c] = up[tm - SUBLANES:, :]
        cw = cw_ref[c]
        y = (cb_ref[c]
             + pltpu.roll(ext, 2, 0)[SUBLANES:] * cw[0:1]
             + pltpu.roll(ext, 1, 0)[SUBLANES:] * cw[1:2]
             + up * cw[2:3])
        a = y[:, :tf]
        g = (a * jax.nn.sigmoid(a) * y[:, tf:]).astype(BF16)
        acc_sc[...] += _dot(g, wdn_ref[c])
        return carry

    lax.fori_loop(0, n_chunks, chunk, 0)
    o_ref[0] = x1_ref[0] + gt_ref[0] * _rms(acc_sc[...], gpost_ref[...])


def _ffn(x1, h2, w_up_r, cw_r, cb_r, w_dn_r, g_post, gt2, tm):
    bsz, seq, d = x1.shape
    n_chunks, _, tf2 = w_up_r.shape
    tf = tf2 // 2
    const = lambda *shape: pl.BlockSpec(shape, lambda b, s: (0,) * len(shape))
    row_spec = pl.BlockSpec((1, tm, d), lambda b, s: (b, s, 0))
    mod_spec = pl.BlockSpec((1, 1, d), lambda b, s: (b, 0, 0))
    kern = functools.partial(_ffn_kernel, n_chunks=n_chunks, tf=tf)
    return pl.pallas_call(
        kern,
        grid=(bsz, seq // tm),
        in_specs=[row_spec, row_spec, const(n_chunks, d, tf2),
                  const(n_chunks, SUBLANES, tf2), const(n_chunks, 1, tf2),
                  const(n_chunks, tf, d), const(1, d), mod_spec],
        out_specs=row_spec,
        out_shape=jax.ShapeDtypeStruct((bsz, seq, d), F32),
        scratch_shapes=[pltpu.VMEM((n_chunks, SUBLANES, tf2), F32),
                        pltpu.VMEM((tm, d), F32)],
        compiler_params=pltpu.CompilerParams(
            dimension_semantics=("arbitrary", "arbitrary"),
            vmem_limit_bytes=VMEM_LIMIT),
        name="ffn",
    )(x1, h2, w_up_r, cw_r, cb_r, w_dn_r, g_post, gt2)


def _rope_tables(seq, scale):
    pos = jnp.arange(seq, dtype=F32)
    inv = ROPE_THETA ** (-jnp.arange(0, ROPE_DIM, 2, dtype=F32) / ROPE_DIM)
    ang = pos[:, None] * inv[None, :]
    cos, sin = jnp.cos(ang) * scale, jnp.sin(ang) * scale
    zeros = jnp.zeros((seq, HALF_ROPE), F32)
    ones = jnp.full((seq, NOPE_DIM), scale, F32)
    tail = jnp.zeros((seq, HEAD_PAD - NOPE_DIM - ROPE_DIM), F32)
    c_tab = jnp.concatenate([ones, cos, cos, tail], axis=1)
    s1_tab = jnp.concatenate([0 * ones, zeros, sin, tail], axis=1)
    s2_tab = jnp.concatenate([0 * ones, -sin, zeros, tail], axis=1)
    return c_tab, s1_tab, s2_tab


def _pad_heads(w, width, start, take):
    r = w.shape[0]
    w = w.reshape(r, MLA_HEADS, width)[:, :, start:start + take]
    return jnp.pad(w, ((0, 0), (0, 0), (0, HEAD_PAD - take))).reshape(r, MLA_HEADS * HEAD_PAD)


def kernel(x, c, w_ada, b_ada, g_pre_mix, g_post_mix, w_in, g_q, w_uq, g_kv, w_ukv,
           gm_ln_g, gm_ln_b, w_spatial, b_spatial, w_out, g_pre_ffn, g_post_ffn,
           w_up, conv_w, conv_b, w_down):
    bsz, seq, d = x.shape
    l = 0
    ada = _ada(c, w_ada[l], b_ada[l]).reshape(bsz, 6, 1, d)
    sh1, sc1, gt1, sh2, sc2, gt2 = (ada[:, i] for i in range(6))

    o1 = Q_LORA
    o2 = o1 + KV_LORA
    o3 = o2 + ROPE_DIM
    wi = w_in[l]
    kr_cols = jnp.pad(wi[:, o2:o3], ((0, 0), (NOPE_DIM, HEAD_PAD - NOPE_DIM - ROPE_DIM)))
    w_in_r = jnp.concatenate([wi[:, :o2], kr_cols, wi[:, o3:]], axis=1).astype(BF16)
    w_uq_p = _pad_heads(w_uq[l], NOPE_DIM + ROPE_DIM, 0, NOPE_DIM + ROPE_DIM).astype(BF16)
    w_k_p = _pad_heads(w_ukv[l], NOPE_DIM + V_DIM, 0, NOPE_DIM)
    w_v = w_ukv[l].reshape(KV_LORA, MLA_HEADS, NOPE_DIM + V_DIM)[:, :, NOPE_DIM:]
    w_kv_p = jnp.concatenate([w_k_p, w_v.reshape(KV_LORA, MLA_HEADS * V_DIM)], axis=1).astype(BF16)
    tabs = _rope_tables(seq, ATTN_SCALE * LOG2E) + _rope_tables(seq, 1.0)
    seg = jnp.arange(GM_WIDTH) // GM_DIM
    a_ln = jnp.where(seg[:, None] == seg[None, :], 1.0 / GM_DIM, 0.0).astype(BF16)
    b_sp = jnp.repeat(b_spatial[l].T, GM_DIM, axis=1)
    row = lambda v: v.reshape(1, -1)

    q, k, v, sgu = _premix(x, sh1, sc1, row(g_pre_mix[l]), w_in_r, row(g_q[l]), w_uq_p,
                           row(g_kv[l]), w_kv_p, tabs, a_ln, row(gm_ln_g[l]),
                           row(gm_ln_b[l]), w_spatial[l], b_sp, tm=512)
    attn = _attention(q, k, v, tq=256, tk=256)
    x1, h2 = _postmix(x, attn, sgu, w_out[l].astype(BF16), row(g_post_mix[l]), gt1,
                      row(g_pre_ffn[l]), sh2, sc2, tm=512)

    tf = 256
    n_chunks = D_FF // tf
    wu = w_up[l].reshape(d, 2, n_chunks, tf).transpose(2, 0, 1, 3).reshape(n_chunks, d, 2 * tf)
    cw = conv_w[l].reshape(CONV_W, 2, n_chunks, tf).transpose(2, 0, 1, 3).reshape(n_chunks, CONV_W, 2 * tf)
    cw = jnp.pad(cw, ((0, 0), (0, SUBLANES - CONV_W), (0, 0)))
    cb = conv_b[l].reshape(2, n_chunks, tf).transpose(1, 0, 2).reshape(n_chunks, 1, 2 * tf)
    wd = w_down[l].reshape(n_chunks, tf, d)
    return _ffn(x1, h2, wu.astype(BF16), cw, cb, wd.astype(BF16), row(g_post_ffn[l]), gt2, tm=512)
```

```python
import functools
import math

import jax
import jax.numpy as jnp
import numpy as np
from jax import lax
from jax.experimental import pallas as pl
from jax.experimental.pallas import tpu as pltpu

D_MODEL = 1024
CHUNK = 64
MLA_HEADS = 8
NOPE_DIM = 64
ROPE_DIM = 32
V_DIM = 64
Q_LORA = 256
KV_LORA = 128
ROPE_THETA = 10000.0
ATTN_SCALE = (NOPE_DIM + ROPE_DIM) ** -0.5
GM_HEADS = 8
GM_DIM = 64
GM_CHUNK = 128
GM_WIDTH = GM_HEADS * GM_DIM
D_FF = 2816
CONV_W = 3
EPS = 1e-6

LANES = 128
SUBLANES = 8
HEAD_PAD = LANES
HALF_ROPE = ROPE_DIM // 2
LOG2E = math.log2(math.e)
VMEM_LIMIT = 56 * 1024 * 1024

BF16 = jnp.bfloat16
F32 = jnp.float32


def _rms(x, g):
    return x * lax.rsqrt(jnp.mean(x * x, axis=-1, keepdims=True) + EPS) * g


def _dot(a, b):
    return jnp.dot(a, b, preferred_element_type=F32)


def _ada_kernel(c_ref, w_ref, b_ref, o_ref):
    c = c_ref[...]
    c_act = c * jax.nn.sigmoid(c)
    o_ref[...] = jnp.dot(c_act, w_ref[...], preferred_element_type=F32,
                         precision=lax.Precision.HIGHEST) + b_ref[...]


def _ada(c, w_ada, b_ada):
    bsz, d = c.shape
    n = w_ada.shape[1]
    tn = 1536
    return pl.pallas_call(
        _ada_kernel,
        grid=(n // tn,),
        in_specs=[pl.BlockSpec((bsz, d), lambda j: (0, 0)),
                  pl.BlockSpec((d, tn), lambda j: (0, j)),
                  pl.BlockSpec((1, tn), lambda j: (0, j))],
        out_specs=pl.BlockSpec((bsz, tn), lambda j: (0, j)),
        out_shape=jax.ShapeDtypeStruct((bsz, n), F32),
        compiler_params=pltpu.CompilerParams(vmem_limit_bytes=VMEM_LIMIT),
        name="ada",
    )(c, w_ada, b_ada.reshape(1, n))


def _rope(t, c_tab, s1_tab, s2_tab):
    return (t * c_tab + pltpu.roll(t, HALF_ROPE, 1) * s1_tab
            + pltpu.roll(t, LANES - HALF_ROPE, 1) * s2_tab)


def _gelu_tanh(x):
    k = math.sqrt(2.0 / math.pi)
    return x * (0.5 * (1.0 + jnp.tanh(k * (x + 0.044715 * (x * x * x)))))


def _premix_kernel(x_ref, sh_ref, sc_ref, gpre_ref, win_ref, gq_ref, wuq_ref,
                   gkv_ref, wkv_ref, cq_ref, s1q_ref, s2q_ref, ck_ref, s1k_ref,
                   s2k_ref, aln_ref, lng_ref, lnb_ref, wsp_ref, bsp_ref,
                   q_ref, k_ref, v_ref, sgu_ref):
    tm = x_ref.shape[1]
    x = x_ref[0]
    h = _rms(x, gpre_ref[...]) * (1.0 + sc_ref[0]) + sh_ref[0]
    z = _dot(h.astype(BF16), win_ref[...])
    o1 = Q_LORA
    o2 = o1 + KV_LORA
    o3 = o2 + HEAD_PAD
    o4 = o3 + GM_WIDTH
    cqn = _rms(z[:, :o1], gq_ref[...]).astype(BF16)
    ckvn = _rms(z[:, o1:o2], gkv_ref[...]).astype(BF16)
    q = _dot(cqn, wuq_ref[...])
    kv = _dot(ckvn, wkv_ref[...])
    krr = _rope(z[:, o2:o3], ck_ref[...], s1k_ref[...], s2k_ref[...])
    cq, s1q, s2q = cq_ref[...], s1q_ref[...], s2q_ref[...]
    for hd in range(MLA_HEADS):
        sl = slice(hd * HEAD_PAD, (hd + 1) * HEAD_PAD)
        q_ref[0, :, sl] = _rope(q[:, sl], cq, s1q, s2q).astype(BF16)
        k_ref[0, :, sl] = (kv[:, sl] + krr).astype(BF16)
    v_ref[0] = kv[:, MLA_HEADS * HEAD_PAD:].astype(BF16)

    u = _gelu_tanh(z[:, o3:o4])
    vv = _gelu_tanh(z[:, o4:])
    aln = aln_ref[...]
    mu = _dot(vv.astype(BF16), aln)
    d = vv - mu
    var = _dot((d * d).astype(BF16), aln)
    y = (d * lax.rsqrt(var + EPS) * lng_ref[...] + lnb_ref[...]).astype(BF16)

    row = lax.broadcasted_iota(jnp.int32, (GM_CHUNK, GM_CHUNK), 0) // CHUNK
    col = lax.broadcasted_iota(jnp.int32, (GM_CHUNK, GM_CHUNK), 1) // CHUNK
    causal = col <= row
    lane = lax.broadcasted_iota(jnp.int32, (GM_CHUNK, LANES), 1)
    first_head = lane < GM_DIM
    bsp = bsp_ref[...]
    for pr in range(GM_HEADS // 2):
        wa = jnp.where(causal, wsp_ref[2 * pr], 0.0)
        wb = jnp.where(causal, wsp_ref[2 * pr + 1], 0.0)
        wpair = jnp.concatenate([wa, wb], axis=0).astype(BF16)
        ls = slice(pr * LANES, (pr + 1) * LANES)
        for blk in range(tm // GM_CHUNK):
            rs = slice(blk * GM_CHUNK, (blk + 1) * GM_CHUNK)
            mixed2 = _dot(wpair, y[rs, ls])
            mixed = jnp.where(first_head, mixed2[:GM_CHUNK], mixed2[GM_CHUNK:])
            sgu_ref[0, rs, ls] = (u[rs, ls] * (mixed + bsp[:, ls])).astype(BF16)


def _premix(x, sh1, sc1, g_pre, w_in_r, g_q, w_uq_p, g_kv, w_kv_p, tabs, a_ln,
            ln_g, ln_b, w_sp, b_sp, tm):
    bsz, seq, d = x.shape
    n_in = w_in_r.shape[1]
    qw = MLA_HEADS * HEAD_PAD
    const = lambda *shape: pl.BlockSpec(shape, lambda s, b: (0,) * len(shape))
    tab_spec = pl.BlockSpec((tm, LANES), lambda s, b: (s, 0))
    row_spec = lambda w: pl.BlockSpec((1, tm, w), lambda s, b: (b, s, 0))
    mod_spec = pl.BlockSpec((1, 1, d), lambda s, b: (b, 0, 0))
    return pl.pallas_call(
        _premix_kernel,
        grid=(seq // tm, bsz),
        in_specs=[row_spec(d), mod_spec, mod_spec, const(1, d), const(d, n_in),
                  const(1, Q_LORA), const(Q_LORA, qw), const(1, KV_LORA),
                  const(KV_LORA, qw + MLA_HEADS * V_DIM)]
                 + [tab_spec] * 6
                 + [const(GM_WIDTH, GM_WIDTH), const(1, GM_WIDTH), const(1, GM_WIDTH),
                    const(GM_HEADS, GM_CHUNK, GM_CHUNK), const(GM_CHUNK, GM_WIDTH)],
        out_specs=[row_spec(qw), row_spec(qw), row_spec(MLA_HEADS * V_DIM),
                   row_spec(GM_WIDTH)],
        out_shape=[jax.ShapeDtypeStruct((bsz, seq, qw), BF16),
                   jax.ShapeDtypeStruct((bsz, seq, qw), BF16),
                   jax.ShapeDtypeStruct((bsz, seq, MLA_HEADS * V_DIM), BF16),
                   jax.ShapeDtypeStruct((bsz, seq, GM_WIDTH), BF16)],
        compiler_params=pltpu.CompilerParams(
            dimension_semantics=("arbitrary", "arbitrary"),
            vmem_limit_bytes=VMEM_LIMIT),
        name="premix",
    )(x, sh1, sc1, g_pre, w_in_r, g_q, w_uq_p, g_kv, w_kv_p, *tabs, a_ln,
      ln_g, ln_b, w_sp, b_sp)


def _attn_kernel(q_ref, k_ref, v_ref, o_ref, m_sc, l_sc, acc_sc, s_sc, *, tq, tk):
    qi = pl.program_id(2)
    lane = lax.broadcasted_iota(jnp.int32, (tq, LANES), 1)
    first_head = lane < V_DIM
    reps = tk // LANES

    m_sc[...] = jnp.full(m_sc.shape, -jnp.inf, F32)
    l_sc[...] = jnp.zeros(l_sc.shape, F32)
    acc_sc[...] = jnp.zeros(acc_sc.shape, F32)

    def score_into(slot, k_blk, mask=None):
        kblk = k_ref[0, pl.ds(pl.multiple_of(k_blk * tk, tk), tk), :]
        for hh in range(2):
            hs = slice(hh * HEAD_PAD, (hh + 1) * HEAD_PAD)
            s = lax.dot_general(q_ref[0, :, hs], kblk[:, hs],
                                (((1,), (1,)), ((), ())),
                                preferred_element_type=F32)
            if mask is not None:
                s = jnp.where(mask, s, -jnp.inf)
            s_sc[slot, hh] = s

    def consume(slot, k_blk):
        vblk = v_ref[0, pl.ds(pl.multiple_of(k_blk * tk, tk), tk), :]
        pvs, alphas = [], []
        for hh in range(2):
            s = s_sc[slot, hh]
            m_prev = m_sc[hh]
            m_new = jnp.maximum(m_prev, jnp.max(s, axis=1, keepdims=True))
            alpha = jnp.exp2(m_prev - m_new)
            p = jnp.exp2(s - pltpu.repeat(m_new, reps, axis=1))
            l_sc[hh] = alpha * l_sc[hh] + jnp.sum(p, axis=1, keepdims=True)
            m_sc[hh] = m_new
            pvs.append(_dot(p.astype(BF16), vblk))
            alphas.append(alpha)
        acc_sc[...] = (acc_sc[...] * jnp.where(first_head, alphas[0], alphas[1])
                       + jnp.where(first_head, pvs[0], pvs[1]))

    qc = lax.broadcasted_iota(jnp.int32, (tq, tk), 0) // CHUNK
    kc = lax.broadcasted_iota(jnp.int32, (tq, tk), 1) // CHUNK
    visible = kc <= qc
    assert tq == tk
    score_into(0, qi, visible)

    def body(jj, carry):
        j0 = 2 * jj
        score_into(1, j0)
        consume(0, jnp.where(jj == 0, qi, j0 - 1))
        score_into(0, j0 + 1)
        consume(1, j0)
        return carry

    pairs = qi // 2
    lax.fori_loop(0, pairs, body, 0)
    in_slot0 = jnp.where(pairs == 0, qi, 2 * pairs - 1)

    @pl.when(qi % 2 == 1)
    def _():
        score_into(1, qi - 1)
        consume(0, in_slot0)
        consume(1, qi - 1)

    @pl.when(qi % 2 == 0)
    def _():
        consume(0, in_slot0)

    inv = jnp.where(first_head, 1.0 / l_sc[0], 1.0 / l_sc[1])
    o_ref[0] = (acc_sc[...] * inv).astype(o_ref.dtype)


def _attention(q, k, v, tq, tk):
    bsz, seq, _ = q.shape
    pairs = MLA_HEADS // 2
    kern = functools.partial(_attn_kernel, tq=tq, tk=tk)
    return pl.pallas_call(
        kern,
        grid=(bsz, pairs, seq // tq),
        in_specs=[pl.BlockSpec((1, tq, 2 * HEAD_PAD), lambda b, p, i: (b, i, p)),
                  pl.BlockSpec((1, seq, 2 * HEAD_PAD), lambda b, p, i: (b, 0, p)),
                  pl.BlockSpec((1, seq, 2 * V_DIM), lambda b, p, i: (b, 0, p))],
        out_specs=pl.BlockSpec((1, tq, 2 * V_DIM), lambda b, p, i: (b, i, p)),
        out_shape=jax.ShapeDtypeStruct((bsz, seq, MLA_HEADS * V_DIM), BF16),
        scratch_shapes=[pltpu.VMEM((2, tq, LANES), F32),
                        pltpu.VMEM((2, tq, LANES), F32),
                        pltpu.VMEM((tq, LANES), F32),
                        pltpu.VMEM((2, 2, tq, tk), F32)],
        compiler_params=pltpu.CompilerParams(
            dimension_semantics=("arbitrary", "arbitrary", "arbitrary"),
            vmem_limit_bytes=VMEM_LIMIT),
        name="attn",
    )(q, k, v)


def _postmix_kernel(x_ref, a_ref, g_ref, wo_ref, gpost_ref, gt_ref, gffn_ref,
                    sh_ref, sc_ref, x1_ref, h2_ref):
    half = a_ref.shape[2]
    m = _dot(a_ref[0], wo_ref[:half, :]) + _dot(g_ref[0], wo_ref[half:, :])
    x1 = x_ref[0] + gt_ref[0] * _rms(m, gpost_ref[...])
    x1_ref[0] = x1
    h2 = _rms(x1, gffn_ref[...]) * (1.0 + sc_ref[0]) + sh_ref[0]
    h2_ref[0] = h2.astype(BF16)


def _postmix(x, attn, sgu, w_out, g_post, gt1, g_ffn, sh2, sc2, tm):
    bsz, seq, d = x.shape
    const = lambda *shape: pl.BlockSpec(shape, lambda b, s: (0,) * len(shape))
    row_spec = lambda w: pl.BlockSpec((1, tm, w), lambda b, s: (b, s, 0))
    mod_spec = pl.BlockSpec((1, 1, d), lambda b, s: (b, 0, 0))
    return pl.pallas_call(
        _postmix_kernel,
        grid=(bsz, seq // tm),
        in_specs=[row_spec(d), row_spec(attn.shape[2]), row_spec(sgu.shape[2]),
                  const(d, d), const(1, d), mod_spec, const(1, d), mod_spec, mod_spec],
        out_specs=[row_spec(d), row_spec(d)],
        out_shape=[jax.ShapeDtypeStruct((bsz, seq, d), F32),
                   jax.ShapeDtypeStruct((bsz, seq, d), BF16)],
        compiler_params=pltpu.CompilerParams(
            dimension_semantics=("arbitrary", "arbitrary"),
            vmem_limit_bytes=VMEM_LIMIT),
        name="postmix",
    )(x, attn, sgu, w_out, g_post, gt1, g_ffn, sh2, sc2)


def _ffn_kernel(x1_ref, h2_ref, wup_ref, cw_ref, cb_ref, wdn_ref, gpost_ref,
                gt_ref, o_ref, carry_sc, acc_sc, *, n_chunks, tf):
    tm = h2_ref.shape[1]

    @pl.when(pl.program_id(1) == 0)
    def _():
        carry_sc[...] = jnp.zeros(carry_sc.shape, F32)

    h2 = h2_ref[0]
    acc_sc[...] = jnp.zeros(acc_sc.shape, F32)

    def chunk(c, carry):
        up = _dot(h2, wup_ref[c])
        ext = jnp.concatenate([carry_sc[c], up], axis=0)
        carry_sc[c] = up[tm - SUBLANES:, :]
        cw = cw_ref[c]
        y = (cb_ref[c]
             + pltpu.roll(ext, 2, 0)[SUBLANES:] * cw[0:1]
             + pltpu.roll(ext, 1, 0)[SUBLANES:] * cw[1:2]
             + up * cw[2:3])
        a = y[:, :tf]
        g = (a * jax.nn.sigmoid(a) * y[:, tf:]).astype(BF16)
        acc_sc[...] += _dot(g, wdn_ref[c])
        return carry

    lax.fori_loop(0, n_chunks, chunk, 0)
    o_ref[0] = x1_ref[0] + gt_ref[0] * _rms(acc_sc[...], gpost_ref[...])


def _ffn(x1, h2, w_up_r, cw_r, cb_r, w_dn_r, g_post, gt2, tm):
    bsz, seq, d = x1.shape
    n_chunks, _, tf2 = w_up_r.shape
    tf = tf2 // 2
    const = lambda *shape: pl.BlockSpec(shape, lambda b, s: (0,) * len(shape))
    row_spec = pl.BlockSpec((1, tm, d), lambda b, s: (b, s, 0))
    mod_spec = pl.BlockSpec((1, 1, d), lambda b, s: (b, 0, 0))
    kern = functools.partial(_ffn_kernel, n_chunks=n_chunks, tf=tf)
    return pl.pallas_call(
        kern,
        grid=(bsz, seq // tm),
        in_specs=[row_spec, row_spec, const(n_chunks, d, tf2),
                  const(n_chunks, SUBLANES, tf2), const(n_chunks, 1, tf2),
                  const(n_chunks, tf, d), const(1, d), mod_spec],
        out_specs=row_spec,
        out_shape=jax.ShapeDtypeStruct((bsz, seq, d), F32),
        scratch_shapes=[pltpu.VMEM((n_chunks, SUBLANES, tf2), F32),
                        pltpu.VMEM((tm, d), F32)],
        compiler_params=pltpu.CompilerParams(
            dimension_semantics=("arbitrary", "arbitrary"),
            vmem_limit_bytes=VMEM_LIMIT),
        name="ffn",
    )(x1, h2, w_up_r, cw_r, cb_r, w_dn_r, g_post, gt2)


def _rope_tables(seq, scale):
    pos = jnp.arange(seq, dtype=F32)
    inv = ROPE_THETA ** (-jnp.arange(0, ROPE_DIM, 2, dtype=F32) / ROPE_DIM)
    ang = pos[:, None] * inv[None, :]
    cos, sin = jnp.cos(ang) * scale, jnp.sin(ang) * scale
    zeros = jnp.zeros((seq, HALF_ROPE), F32)
    ones = jnp.full((seq, NOPE_DIM), scale, F32)
    tail = jnp.zeros((seq, HEAD_PAD - NOPE_DIM - ROPE_DIM), F32)
    c_tab = jnp.concatenate([ones, cos, cos, tail], axis=1)
    s1_tab = jnp.concatenate([0 * ones, zeros, sin, tail], axis=1)
    s2_tab = jnp.concatenate([0 * ones, -sin, zeros, tail], axis=1)
    return c_tab, s1_tab, s2_tab


def _pad_heads(w, width, start, take):
    r = w.shape[0]
    w = w.reshape(r, MLA_HEADS, width)[:, :, start:start + take]
    return jnp.pad(w, ((0, 0), (0, 0), (0, HEAD_PAD - take))).reshape(r, MLA_HEADS * HEAD_PAD)


def kernel(x, c, w_ada, b_ada, g_pre_mix, g_post_mix, w_in, g_q, w_uq, g_kv, w_ukv,
           gm_ln_g, gm_ln_b, w_spatial, b_spatial, w_out, g_pre_ffn, g_post_ffn,
           w_up, conv_w, conv_b, w_down):
    bsz, seq, d = x.shape
    l = 0
    ada = _ada(c, w_ada[l], b_ada[l]).reshape(bsz, 6, 1, d)
    sh1, sc1, gt1, sh2, sc2, gt2 = (ada[:, i] for i in range(6))

    o1 = Q_LORA
    o2 = o1 + KV_LORA
    o3 = o2 + ROPE_DIM
    wi = w_in[l]
    kr_cols = jnp.pad(wi[:, o2:o3], ((0, 0), (NOPE_DIM, HEAD_PAD - NOPE_DIM - ROPE_DIM)))
    w_in_r = jnp.concatenate([wi[:, :o2], kr_cols, wi[:, o3:]], axis=1).astype(BF16)
    w_uq_p = _pad_heads(w_uq[l], NOPE_DIM + ROPE_DIM, 0, NOPE_DIM + ROPE_DIM).astype(BF16)
    w_k_p = _pad_heads(w_ukv[l], NOPE_DIM + V_DIM, 0, NOPE_DIM)
    w_v = w_ukv[l].reshape(KV_LORA, MLA_HEADS, NOPE_DIM + V_DIM)[:, :, NOPE_DIM:]
    w_kv_p = jnp.concatenate([w_k_p, w_v.reshape(KV_LORA, MLA_HEADS * V_DIM)], axis=1).astype(BF16)
    tabs = _rope_tables(seq, ATTN_SCALE * LOG2E) + _rope_tables(seq, 1.0)
    seg = jnp.arange(GM_WIDTH) // GM_DIM
    a_ln = jnp.where(seg[:, None] == seg[None, :], 1.0 / GM_DIM, 0.0).astype(BF16)
    b_sp = jnp.repeat(b_spatial[l].T, GM_DIM, axis=1)
    row = lambda v: v.reshape(1, -1)

    q, k, v, sgu = _premix(x, sh1, sc1, row(g_pre_mix[l]), w_in_r, row(g_q[l]), w_uq_p,
                           row(g_kv[l]), w_kv_p, tabs, a_ln, row(gm_ln_g[l]),
                           row(gm_ln_b[l]), w_spatial[l], b_sp, tm=512)
    attn = _attention(q, k, v, tq=512, tk=512)
    x1, h2 = _postmix(x, attn, sgu, w_out[l].astype(BF16), row(g_post_mix[l]), gt1,
                      row(g_pre_ffn[l]), sh2, sc2, tm=512)

    tf = 256
    n_chunks = D_FF // tf
    wu = w_up[l].reshape(d, 2, n_chunks, tf).transpose(2, 0, 1, 3).reshape(n_chunks, d, 2 * tf)
    cw = conv_w[l].reshape(CONV_W, 2, n_chunks, tf).transpose(2, 0, 1, 3).reshape(n_chunks, CONV_W, 2 * tf)
    cw = jnp.pad(cw, ((0, 0), (0, SUBLANES - CONV_W), (0, 0)))
    cb = conv_b[l].reshape(2, n_chunks, tf).transpose(1, 0, 2).reshape(n_chunks, 1, 2 * tf)
    wd = w_down[l].reshape(n_chunks, tf, d)
    return _ffn(x1, h2, wu.astype(BF16), cw, cb, wd.astype(BF16), row(g_post_ffn[l]), gt2, tm=512)
```
